```python
import math
import jax
import jax.numpy as jnp
from jax import lax
import numpy as np

D_MODEL = 1024
BATCH = 8
SEQ = 4096
DEPTH = 2

N_MIXERS = 2
RMS_EPS = 1e-6

GLA_HEADS = 4
GLA_DK = D_MODEL // 2
GLA_DV = D_MODEL
GLA_DK_HEAD = GLA_DK // GLA_HEADS
GLA_DV_HEAD = GLA_DV // GLA_HEADS
GLA_GATE_RANK = 16
GLA_GATE_TEMP = 16.0
GLA_CHUNK = 64
GLA_IN_COLS = 2 * GLA_DK + 2 * GLA_DV + GLA_GATE_RANK

LRU_WIDTH = D_MODEL
LRU_BLOCKS = 4
LRU_BLOCK_W = LRU_WIDTH // LRU_BLOCKS
LRU_CONV_W = 4
LRU_C = 8.0

MOE_GROUPS = 4
MOE_EXPERTS_PER_GROUP = 8
MOE_EXPERTS = MOE_GROUPS * MOE_EXPERTS_PER_GROUP
MOE_TOPK = 2
MOE_FF = D_MODEL // 4
MOE_BLOCK = 128

N_GLA_LAYERS = (DEPTH + 1) // 2
N_LRU_LAYERS = DEPTH // 2

kernel_name = "hybrid_gla_rglru_hier_moe"


def rmsnorm(x, w):
    xf = x.astype(jnp.float32)
    y = xf * lax.rsqrt(jnp.mean(xf * xf, axis=-1, keepdims=True) + RMS_EPS)
    return (y * w.astype(jnp.float32)).astype(x.dtype)


def gla_mixer(x, norm_w, w_in, w_alpha, b_alpha, head_norm_w, w_out):
    bsz, seq, _ = x.shape
    n_chunks = seq // GLA_CHUNK
    h = rmsnorm(x, norm_w)
    proj = h @ w_in
    q, k, v, g, lr = jnp.split(
        proj, [GLA_DK, 2 * GLA_DK, 2 * GLA_DK + GLA_DV, 2 * GLA_DK + 2 * GLA_DV], axis=-1)
    log_a = jax.nn.log_sigmoid((lr @ w_alpha + b_alpha).astype(jnp.float32)) / GLA_GATE_TEMP

    def to_chunks(t, d):
        t = t.reshape(bsz, n_chunks, GLA_CHUNK, GLA_HEADS, d)
        return t.transpose(0, 3, 1, 2, 4).astype(jnp.float32)

    q = to_chunks(q, GLA_DK_HEAD) * (GLA_DK_HEAD ** -0.5)
    k = to_chunks(k, GLA_DK_HEAD)
    v = to_chunks(v, GLA_DV_HEAD)
    la = to_chunks(log_a, GLA_DK_HEAD)
    cum = jnp.cumsum(la, axis=3)
    ref = cum[:, :, :, GLA_CHUNK // 2:GLA_CHUNK // 2 + 1]

    q_in = q * jnp.exp(cum - ref)
    k_in = k * jnp.exp(ref - cum)
    causal = jnp.tril(jnp.ones((GLA_CHUNK, GLA_CHUNK), dtype=bool))
    scores = jnp.einsum('bhncd,bhnsd->bhncs', q_in, k_in)
    scores = jnp.where(causal, scores, 0.0)
    o_intra = jnp.einsum('bhncs,bhnse->bhnce', scores, v)

    last = cum[:, :, :, -1]
    k_st = k * jnp.exp(last[:, :, :, None, :] - cum)
    u = jnp.einsum('bhncd,bhnce->bhnde', k_st, v)

    def step(state, inp):
        decay, u_n = inp
        return decay[..., None] * state + u_n, state

    state0 = jnp.zeros((bsz, GLA_HEADS, GLA_DK_HEAD, GLA_DV_HEAD), jnp.float32)
    _, s_prev = lax.scan(step, state0, (jnp.moveaxis(jnp.exp(last), 2, 0), jnp.moveaxis(u, 2, 0)))
    s_prev = jnp.moveaxis(s_prev, 0, 2)
    o_inter = jnp.einsum('bhncd,bhnde->bhnce', q * jnp.exp(cum), s_prev)

    o = (o_intra + o_inter).transpose(0, 2, 3, 1, 4).reshape(bsz, seq, GLA_HEADS, GLA_DV_HEAD)
    o = rmsnorm(o, head_norm_w)
    o = o.reshape(bsz, seq, GLA_DV) * jax.nn.silu(g.astype(jnp.float32))
    return o.astype(x.dtype) @ w_out


def _lru_combine(left, right):
    a_l, b_l = left
    a_r, b_r = right
    return a_l * a_r, a_r * b_l + b_r


def rglru_mixer(x, norm_w, w_in, conv_w, conv_b, w_rgate, b_rgate, w_igate, b_igate, lam, w_out):
    bsz, seq, _ = x.shape
    h = rmsnorm(x, norm_w)
    xb, gb = jnp.split(h @ w_in, 2, axis=-1)
    gate = jax.nn.gelu(gb, approximate=True)
    xp = jnp.pad(xb, ((0, 0), (LRU_CONV_W - 1, 0), (0, 0)))
    xc = conv_b + xp[:, 0:seq] * conv_w[0]
    for tap in range(1, LRU_CONV_W):
        xc = xc + xp[:, tap:tap + seq] * conv_w[tap]
    xr = xc.reshape(bsz, seq, LRU_BLOCKS, LRU_BLOCK_W)
    r = jax.nn.sigmoid(jnp.einsum('bsni,nij->bsnj', xr, w_rgate).reshape(bsz, seq, LRU_WIDTH) + b_rgate)
    i = jax.nn.sigmoid(jnp.einsum('bsni,nij->bsnj', xr, w_igate).reshape(bsz, seq, LRU_WIDTH) + b_igate)
    log_a = -LRU_C * r.astype(jnp.float32) * jax.nn.softplus(-lam.astype(jnp.float32))
    a = jnp.exp(log_a)
    mult = jnp.sqrt(-jnp.expm1(2.0 * log_a))
    xn = xc.astype(jnp.float32) * i.astype(jnp.float32) * mult
    _, hs = lax.associative_scan(_lru_combine, (a, xn), axis=1)
    y = hs.astype(x.dtype) * gate
    return y @ w_out


def hier_moe(x, norm_w, w_group, w_expert, w_gate, w_up, w_down):
    bsz, seq, d = x.shape
    n_tok = bsz * seq
    h = rmsnorm(x, norm_w).reshape(n_tok, d)
    grp_logits = (h @ w_group).astype(jnp.float32)
    grp_prob = jax.nn.softmax(grp_logits, axis=-1)
    g_sel = jnp.argmax(grp_logits, axis=-1)
    p_g = jnp.take_along_axis(grp_prob, g_sel[:, None], axis=-1)[:, 0]
    exp_logits = (h @ w_expert).astype(jnp.float32).reshape(n_tok, MOE_GROUPS, MOE_EXPERTS_PER_GROUP)
    exp_logits = jnp.take_along_axis(exp_logits, g_sel[:, None, None], axis=1)[:, 0]
    exp_prob = jax.nn.softmax(exp_logits, axis=-1)
    top_p, top_e = lax.top_k(exp_prob, MOE_TOPK)
    top_p = top_p / jnp.sum(top_p, axis=-1, keepdims=True)
    weights = p_g[:, None] * top_p
    eid = g_sel[:, None] * MOE_EXPERTS_PER_GROUP + top_e

    n_assign = n_tok * MOE_TOPK
    eid_f = eid.reshape(-1).astype(jnp.int32)
    w_f = weights.reshape(-1)
    tok_f = jnp.repeat(jnp.arange(n_tok, dtype=jnp.int32), MOE_TOPK)
    order = jnp.argsort(eid_f)
    s_eid, s_tok, s_w = eid_f[order], tok_f[order], w_f[order]
    counts = jnp.bincount(eid_f, length=MOE_EXPERTS)
    starts = jnp.cumsum(counts) - counts
    pcounts = ((counts + MOE_BLOCK - 1) // MOE_BLOCK) * MOE_BLOCK
    pends = jnp.cumsum(pcounts)
    pstarts = pends - pcounts
    dest = pstarts[s_eid] + (jnp.arange(n_assign, dtype=jnp.int32) - starts[s_eid])
    n_blocks = (n_assign + MOE_BLOCK - 1) // MOE_BLOCK + MOE_EXPERTS
    n_rows = n_blocks * MOE_BLOCK
    buf_tok = jnp.full((n_rows,), n_tok, jnp.int32).at[dest].set(s_tok)
    buf_w = jnp.zeros((n_rows,), jnp.float32).at[dest].set(s_w)
    blk_start = jnp.arange(n_blocks, dtype=jnp.int32) * MOE_BLOCK
    blk_eid = jnp.minimum(jnp.searchsorted(pends, blk_start, side='right'), MOE_EXPERTS - 1)
    h_pad = jnp.concatenate([h, jnp.zeros((1, d), h.dtype)], axis=0)
    xb = h_pad[buf_tok].reshape(n_blocks, MOE_BLOCK, d)

    def expert_block(args):
        xblk, e = args
        return (jax.nn.silu(xblk @ w_gate[e]) * (xblk @ w_up[e])) @ w_down[e]

    yb = lax.map(expert_block, (xb, blk_eid)).reshape(n_rows, d)
    yb = yb * buf_w[:, None].astype(yb.dtype)
    out = jnp.zeros((n_tok + 1, d), yb.dtype).at[buf_tok].add(yb)[:n_tok]
    return out.reshape(bsz, seq, d).astype(x.dtype)


def setup_inputs(seed: int = 0) -> dict:
    key = jax.random.key(seed)
    ks = jax.random.split(key, 32)
    f32 = jnp.float32

    def nrm(k, shape, fan_in):
        return jax.random.normal(k, shape, f32) * (fan_in ** -0.5)

    def gain(k, shape):
        return 1.0 + 0.02 * jax.random.normal(k, shape, f32)

    def small(k, shape, s=0.02):
        return s * jax.random.normal(k, shape, f32)

    ng, nl, L = N_GLA_LAYERS, N_LRU_LAYERS, DEPTH
    u = jax.random.uniform(ks[18], (nl, LRU_WIDTH), f32, minval=0.9, maxval=0.999)
    s = u ** (1.0 / LRU_C)
    lam = jnp.log(s) - jnp.log1p(-s)
    return {
        "x": jax.random.normal(ks[0], (BATCH, SEQ, D_MODEL), f32),
        "gla_norm": gain(ks[1], (ng, D_MODEL)),
        "gla_w_in": nrm(ks[2], (ng, D_MODEL, GLA_IN_COLS), D_MODEL),
        "gla_w_alpha": nrm(ks[3], (ng, GLA_GATE_RANK, GLA_DK), GLA_GATE_RANK),
        "gla_b_alpha": small(ks[4], (ng, GLA_DK), 0.1),
        "gla_head_norm": gain(ks[5], (ng, GLA_DV_HEAD)),
        "gla_w_out": nrm(ks[6], (ng, GLA_DV, D_MODEL), GLA_DV),
        "lru_norm": gain(ks[7], (nl, D_MODEL)),
        "lru_w_in": nrm(ks[8], (nl, D_MODEL, 2 * LRU_WIDTH), D_MODEL),
        "lru_conv_w": nrm(ks[9], (nl, LRU_CONV_W, LRU_WIDTH), LRU_CONV_W),
        "lru_conv_b": small(ks[10], (nl, LRU_WIDTH)),
        "lru_w_rgate": nrm(ks[11], (nl, LRU_BLOCKS, LRU_BLOCK_W, LRU_BLOCK_W), LRU_BLOCK_W),
        "lru_b_rgate": small(ks[12], (nl, LRU_WIDTH)),
        "lru_w_igate": nrm(ks[13], (nl, LRU_BLOCKS, LRU_BLOCK_W, LRU_BLOCK_W), LRU_BLOCK_W),
        "lru_b_igate": small(ks[14], (nl, LRU_WIDTH)),
        "lru_lambda": lam,
        "lru_w_out": nrm(ks[15], (nl, LRU_WIDTH, D_MODEL), LRU_WIDTH),
        "moe_norm": gain(ks[16], (L, D_MODEL)),
        "moe_w_group": nrm(ks[17], (L, D_MODEL, MOE_GROUPS), D_MODEL),
        "moe_w_expert": nrm(ks[19], (L, D_MODEL, MOE_EXPERTS), D_MODEL),
        "moe_w_gate": nrm(ks[20], (L, MOE_EXPERTS, D_MODEL, MOE_FF), D_MODEL),
        "moe_w_up": nrm(ks[21], (L, MOE_EXPERTS, D_MODEL, MOE_FF), D_MODEL),
        "moe_w_down": nrm(ks[22], (L, MOE_EXPERTS, MOE_FF, D_MODEL), MOE_FF),
        "final_norm": gain(ks[23], (D_MODEL,)),
    }


def reference(x, gla_norm, gla_w_in, gla_w_alpha, gla_b_alpha, gla_head_norm, gla_w_out,
              lru_norm, lru_w_in, lru_conv_w, lru_conv_b, lru_w_rgate, lru_b_rgate,
              lru_w_igate, lru_b_igate, lru_lambda, lru_w_out,
              moe_norm, moe_w_group, moe_w_expert, moe_w_gate, moe_w_up, moe_w_down,
              final_norm):
    for layer in range(DEPTH):
        j = layer // N_MIXERS
        if layer % N_MIXERS == 0:
            x = x + gla_mixer(x, gla_norm[j], gla_w_in[j], gla_w_alpha[j], gla_b_alpha[j],
                              gla_head_norm[j], gla_w_out[j])
        else:
            x = x + rglru_mixer(x, lru_norm[j], lru_w_in[j], lru_conv_w[j], lru_conv_b[j],
                                lru_w_rgate[j], lru_b_rgate[j], lru_w_igate[j], lru_b_igate[j],
                                lru_lambda[j], lru_w_out[j])
        x = x + hier_moe(x, moe_norm[layer], moe_w_group[layer], moe_w_expert[layer],
                         moe_w_gate[layer], moe_w_up[layer], moe_w_down[layer])
    return rmsnorm(x, final_norm)
```

```python
import functools

import jax
import jax.numpy as jnp
from jax import lax
from jax.experimental import pallas as pl
from jax.experimental.pallas import tpu as pltpu

F32 = jnp.float32
BF16 = jnp.bfloat16

RMS_EPS = 1e-6
D_MODEL = 1024

GLA_HEADS = 4
GLA_DK = 512
GLA_DV = 1024
GLA_DK_HEAD = 128
GLA_DV_HEAD = 256
GLA_GATE_RANK = 16
GLA_GATE_TEMP = 16.0
GLA_CHUNK = 64

LRU_WIDTH = 1024
LRU_BLOCKS = 4
LRU_BLOCK_W = 256
LRU_CONV_W = 4
LRU_C = 8.0

MOE_GROUPS = 4
MOE_EXPERTS_PER_GROUP = 8
MOE_EXPERTS = 32
MOE_TOPK = 2
MOE_FF = 256

LANES = 128
SUBLANES = 8
VMEM_LIMIT = 48 * 1024 * 1024

ROW_TILE = 512
GLA_SEQ_TILE = 512
LRU_SEQ_TILE = 256
MOE_ROW_BLOCK = 256


def _params(*sem):
    return pltpu.CompilerParams(dimension_semantics=sem, vmem_limit_bytes=VMEM_LIMIT)


def _rmsnorm(x, w):
    return x * lax.rsqrt(jnp.mean(x * x, axis=-1, keepdims=True) + RMS_EPS) * w


def _dot(a, b):
    return jnp.dot(a, b, preferred_element_type=F32)


def _split3(x):
    hi = x.astype(BF16)
    r1 = x - hi.astype(F32)
    mid = r1.astype(BF16)
    lo = (r1 - mid.astype(F32)).astype(BF16)
    return hi, mid, lo


def _gla_proj_kernel(x_ref, nw_ref, w_ref, wlr_ref, wal_ref, bal_ref,
                     q_ref, k_ref, v_ref, g_ref, la_ref):
    hb = _rmsnorm(x_ref[...], nw_ref[...]).astype(BF16)
    q_ref[...] = _dot(hb, w_ref[:, 0:GLA_DK])
    k_ref[...] = _dot(hb, w_ref[:, GLA_DK:2 * GLA_DK])
    v_ref[...] = _dot(hb, w_ref[:, 2 * GLA_DK:2 * GLA_DK + GLA_DV]).astype(BF16)
    g_ref[...] = _dot(hb, w_ref[:, 2 * GLA_DK + GLA_DV:2 * GLA_DK + 2 * GLA_DV])
    lr = _dot(hb, wlr_ref[...])
    z = _dot(lr.astype(BF16), wal_ref[...]) + bal_ref[...]
    la_ref[...] = (jnp.minimum(z, 0.0) - jnp.log1p(jnp.exp(-jnp.abs(z)))) / GLA_GATE_TEMP


def _gla_proj(x2, norm_w, w_in, w_alpha, b_alpha):
    n_tok = x2.shape[0]
    n_main = 2 * GLA_DK + 2 * GLA_DV
    w_main = w_in[:, :n_main].astype(BF16)
    w_lr = jnp.pad(w_in[:, n_main:], ((0, 0), (0, LANES - GLA_GATE_RANK))).astype(BF16)
    w_al = jnp.pad(w_alpha, ((0, LANES - GLA_GATE_RANK), (0, 0))).astype(BF16)
    tm = ROW_TILE
    row = lambda i: (i, 0)
    fixed = lambda i: (0, 0)
    return pl.pallas_call(
        _gla_proj_kernel,
        grid=(n_tok // tm,),
        in_specs=[
            pl.BlockSpec((tm, D_MODEL), row),
            pl.BlockSpec((1, D_MODEL), fixed),
            pl.BlockSpec((D_MODEL, n_main), fixed),
            pl.BlockSpec((D_MODEL, LANES), fixed),
            pl.BlockSpec((LANES, GLA_DK), fixed),
            pl.BlockSpec((1, GLA_DK), fixed),
        ],
        out_specs=[
            pl.BlockSpec((tm, GLA_DK), row),
            pl.BlockSpec((tm, GLA_DK), row),
            pl.BlockSpec((tm, GLA_DV), row),
            pl.BlockSpec((tm, GLA_DV), row),
            pl.BlockSpec((tm, GLA_DK), row),
        ],
        out_shape=[
            jax.ShapeDtypeStruct((n_tok, GLA_DK), F32),
            jax.ShapeDtypeStruct((n_tok, GLA_DK), F32),
            jax.ShapeDtypeStruct((n_tok, GLA_DV), BF16),
            jax.ShapeDtypeStruct((n_tok, GLA_DV), F32),
            jax.ShapeDtypeStruct((n_tok, GLA_DK), F32),
        ],
        compiler_params=_params("parallel"),
        name="gla_proj",
    )(x2, norm_w.reshape(1, D_MODEL), w_main, w_lr, w_al, b_alpha.reshape(1, GLA_DK))


def _gla_chunk_kernel(q_ref, k_ref, v_ref, g_ref, la_ref, hnw_ref, y_ref, state_ref):
    @pl.when(pl.program_id(2) == 0)
    def _():
        state_ref[...] = jnp.zeros_like(state_ref)

    c = GLA_CHUNK
    rows = lax.broadcasted_iota(jnp.int32, (c, c), 0)
    cols = lax.broadcasted_iota(jnp.int32, (c, c), 1)
    causal = cols <= rows
    tri = jnp.where(causal, 1.0, 0.0).astype(BF16)
    scale = GLA_DK_HEAD ** -0.5
    hnw = hnw_ref[...]
    state_t = state_ref[...]

    for ci in range(q_ref.shape[0] // c):
        sl = pl.ds(ci * c, c)
        q = q_ref[sl, :] * scale
        k = k_ref[sl, :]
        v = v_ref[sl, :]
        la_hi, la_mid, la_lo = _split3(la_ref[sl, :])
        cum = _dot(tri, la_hi) + _dot(tri, la_mid) + _dot(tri, la_lo)
        mid = cum[c // 2:c // 2 + 1, :]
        last = cum[c - 1:c, :]
        q_in = (q * jnp.exp(cum - mid)).astype(BF16)
        k_in = (k * jnp.exp(mid - cum)).astype(BF16)
        scores = lax.dot_general(q_in, k_in, (((1,), (1,)), ((), ())), preferred_element_type=F32)
        scores = jnp.where(causal, scores, 0.0).astype(BF16)
        o = _dot(scores, v)
        q_dec = (q * jnp.exp(cum)).astype(BF16)
        o = o + lax.dot_general(q_dec, state_t.astype(BF16), (((1,), (1,)), ((), ())),
                                preferred_element_type=F32)
        k_st = (k * jnp.exp(last - cum)).astype(BF16)
        u_t = lax.dot_general(v, k_st, (((0,), (0,)), ((), ())), preferred_element_type=F32)
        state_t = state_t * jnp.exp(last) + u_t
        g = g_ref[sl, :]
        y = _rmsnorm(o, hnw) * (g * jax.nn.sigmoid(g))
        y_ref[sl, :] = y.astype(y_ref.dtype)

    state_ref[...] = state_t


def _gla_chunk(q, k, v, g, la, head_norm_w, bsz, seq):
    ts = GLA_SEQ_TILE
    nt = seq // ts
    blk = lambda b, h, t: (b * nt + t, h)
    return pl.pallas_call(
        _gla_chunk_kernel,
        grid=(bsz, GLA_HEADS, nt),
        in_specs=[
            pl.BlockSpec((ts, GLA_DK_HEAD), blk),
            pl.BlockSpec((ts, GLA_DK_HEAD), blk),
            pl.BlockSpec((ts, GLA_DV_HEAD), blk),
            pl.BlockSpec((ts, GLA_DV_HEAD), blk),
            pl.BlockSpec((ts, GLA_DK_HEAD), blk),
            pl.BlockSpec((1, GLA_DV_HEAD), lambda b, h, t: (0, 0)),
        ],
        out_specs=pl.BlockSpec((ts, GLA_DV_HEAD), blk),
        out_shape=jax.ShapeDtypeStruct((bsz * seq, GLA_DV), BF16),
        scratch_shapes=[pltpu.VMEM((GLA_DV_HEAD, GLA_DK_HEAD), F32)],
        compiler_params=_params("parallel", "parallel", "arbitrary"),
        name="gla_chunk",
    )(q, k, v, g, la, head_norm_w.reshape(1, GLA_DV_HEAD))


def _matmul_residual_kernel(y_ref, w_ref, res_ref, o_ref):
    o_ref[...] = res_ref[...] + _dot(y_ref[...], w_ref[...])


def _matmul_residual(y, w, res):
    n_tok, kdim = y.shape
    tm = ROW_TILE
    row = lambda i: (i, 0)
    return pl.pallas_call(
        _matmul_residual_kernel,
        grid=(n_tok // tm,),
        in_specs=[
            pl.BlockSpec((tm, kdim), row),
            pl.BlockSpec((kdim, D_MODEL), lambda i: (0, 0)),
            pl.BlockSpec((tm, D_MODEL), row),
        ],
        out_specs=pl.BlockSpec((tm, D_MODEL), row),
        out_shape=jax.ShapeDtypeStruct((n_tok, D_MODEL), F32),
        compiler_params=_params("parallel"),
        name="matmul_residual",
    )(y, w.astype(BF16), res)


def _router_kernel(x_ref, nw_ref, whi_ref, wlo_ref, h_ref, ids_ref, wts_ref):
    h = _rmsnorm(x_ref[...], nw_ref[...])
    h_hi = h.astype(BF16)
    h_ref[...] = h_hi
    h_lo = (h - h_hi.astype(F32)).astype(BF16)
    logits = _dot(h_hi, whi_ref[...]) + _dot(h_lo, whi_ref[...]) + _dot(h_hi, wlo_ref[...])
    col = lax.broadcasted_iota(jnp.int32, logits.shape, 1)
    neg = -jnp.inf

    gl = jnp.where(col < MOE_GROUPS, logits, neg)
    gmax = jnp.max(gl, axis=-1, keepdims=True)
    g_sel = jnp.min(jnp.where(gl == gmax, col, LANES), axis=-1, keepdims=True)
    p_g = 1.0 / jnp.sum(jnp.exp(gl - gmax), axis=-1, keepdims=True)

    lo = MOE_GROUPS + MOE_EXPERTS_PER_GROUP * g_sel
    el = jnp.where((col >= lo) & (col < lo + MOE_EXPERTS_PER_GROUP), logits, neg)
    m1 = jnp.max(el, axis=-1, keepdims=True)
    i1 = jnp.min(jnp.where(el == m1, col, LANES), axis=-1, keepdims=True)
    el2 = jnp.where(col == i1, neg, el)
    m2 = jnp.max(el2, axis=-1, keepdims=True)
    i2 = jnp.min(jnp.where(el2 == m2, col, LANES), axis=-1, keepdims=True)
    e2 = jnp.exp(m2 - m1)
    w1 = p_g / (1.0 + e2)
    w2 = p_g * e2 / (1.0 + e2)

    ids_ref[...] = jnp.where(col == 0, i1 - MOE_GROUPS, jnp.where(col == 1, i2 - MOE_GROUPS, 0))
    wts_ref[...] = jnp.where(col == 0, w1, jnp.where(col == 1, w2, 0.0))


def _router(x2, norm_w, w_group, w_expert):
    n_tok = x2.shape[0]
    w_r = jnp.concatenate([w_group, w_expert], axis=1)
    w_r = jnp.pad(w_r, ((0, 0), (0, LANES - w_r.shape[1])))
    w_hi = w_r.astype(BF16)
    w_lo = (w_r - w_hi.astype(F32)).astype(BF16)
    tm = ROW_TILE
    row = lambda i: (i, 0)
    fixed = lambda i: (0, 0)
    return pl.pallas_call(
        _router_kernel,
        grid=(n_tok // tm,),
        in_specs=[
            pl.BlockSpec((tm, D_MODEL), row),
            pl.BlockSpec((1, D_MODEL), fixed),
            pl.BlockSpec((D_MODEL, LANES), fixed),
            pl.BlockSpec((D_MODEL, LANES), fixed),
        ],
        out_specs=[
            pl.BlockSpec((tm, D_MODEL), row),
            pl.BlockSpec((tm, LANES), row),
            pl.BlockSpec((tm, LANES), row),
        ],
        out_shape=[
            jax.ShapeDtypeStruct((n_tok, D_MODEL), BF16),
            jax.ShapeDtypeStruct((n_tok, LANES), jnp.int32),
            jax.ShapeDtypeStruct((n_tok, LANES), F32),
        ],
        compiler_params=_params("parallel"),
        name="moe_router",
    )(x2, norm_w.reshape(1, D_MODEL), w_hi, w_lo)


def _expert_kernel(eid_ref, nused_ref, x_ref, wgu_ref, wd_ref, y_ref):
    i = pl.program_id(0)

    @pl.when(i < nused_ref[0])
    def _():
        gu = _dot(x_ref[...], wgu_ref[...])
        gate = gu[:, :MOE_FF]
        act = (gate * jax.nn.sigmoid(gate)) * gu[:, MOE_FF:]
        y_ref[...] = _dot(act.astype(BF16), wd_ref[...]).astype(y_ref.dtype)

    @pl.when(i >= nused_ref[0])
    def _():
        y_ref[...] = jnp.zeros_like(y_ref)


def _experts(xb, blk_eid, n_used, w_gu, w_d):
    n_rows = xb.shape[0]
    rb = MOE_ROW_BLOCK
    grid_spec = pltpu.PrefetchScalarGridSpec(
        num_scalar_prefetch=2,
        grid=(n_rows // rb,),
        in_specs=[
            pl.BlockSpec((rb, D_MODEL), lambda i, eid, nu: (i, 0)),
            pl.BlockSpec((None, D_MODEL, 2 * MOE_FF), lambda i, eid, nu: (eid[i], 0, 0)),
            pl.BlockSpec((None, MOE_FF, D_MODEL), lambda i, eid, nu: (eid[i], 0, 0)),
        ],
        out_specs=pl.BlockSpec((rb, D_MODEL), lambda i, eid, nu: (i, 0)),
    )
    return pl.pallas_call(
        _expert_kernel,
        grid_spec=grid_spec,
        out_shape=jax.ShapeDtypeStruct((n_rows, D_MODEL), BF16),
        compiler_params=_params("arbitrary"),
        name="moe_experts",
    )(blk_eid, n_used, xb, w_gu, w_d)


def _combine_kernel(x_ref, y0_ref, y1_ref, wts_ref, o_ref):
    w = wts_ref[...]
    o_ref[...] = (x_ref[...] + w[:, 0:1] * y0_ref[...].astype(F32)
                  + w[:, 1:2] * y1_ref[...].astype(F32))


def _combine_norm_kernel(x_ref, y0_ref, y1_ref, wts_ref, nw_ref, o_ref):
    w = wts_ref[...]
    out = (x_ref[...] + w[:, 0:1] * y0_ref[...].astype(F32)
           + w[:, 1:2] * y1_ref[...].astype(F32))
    o_ref[...] = _rmsnorm(out, nw_ref[...])


def _combine(x2, y0, y1, wts, final_norm_w=None):
    n_tok = x2.shape[0]
    tm = ROW_TILE
    row = lambda i: (i, 0)
    in_specs = [
        pl.BlockSpec((tm, D_MODEL), row),
        pl.BlockSpec((tm, D_MODEL), row),
        pl.BlockSpec((tm, D_MODEL), row),
        pl.BlockSpec((tm, LANES), row),
    ]
    args = [x2, y0, y1, wts]
    body = _combine_kernel
    if final_norm_w is not None:
        in_specs.append(pl.BlockSpec((1, D_MODEL), lambda i: (0, 0)))
        args.append(final_norm_w.reshape(1, D_MODEL))
        body = _combine_norm_kernel
    return pl.pallas_call(
        body,
        grid=(n_tok // tm,),
        in_specs=in_specs,
        out_specs=pl.BlockSpec((tm, D_MODEL), row),
        out_shape=jax.ShapeDtypeStruct((n_tok, D_MODEL), F32),
        compiler_params=_params("parallel"),
        name="moe_combine",
    )(*args)


def _hier_moe(x2, norm_w, w_group, w_expert, w_gate, w_up, w_down, final_norm_w=None):
    n_tok = x2.shape[0]
    rb = MOE_ROW_BLOCK
    h, ids, wts = _router(x2, norm_w, w_group, w_expert)
    eid = ids[:, :MOE_TOPK]

    n_assign = n_tok * MOE_TOPK
    eid_f = eid.reshape(-1)
    onehot = (eid_f[:, None] == jnp.arange(MOE_EXPERTS, dtype=jnp.int32)[None, :]).astype(jnp.int32)
    csum = jnp.cumsum(onehot, axis=0)
    counts = csum[-1]
    rank = jnp.take_along_axis(csum, eid_f[:, None], axis=1)[:, 0] - 1
    pcounts = ((counts + rb - 1) // rb) * rb
    pends = jnp.cumsum(pcounts)
    pstarts = pends - pcounts
    dest = (pstarts[eid_f] + rank).astype(jnp.int32)
    n_blocks = n_assign // rb + MOE_EXPERTS
    n_rows = n_blocks * rb
    tok_f = jnp.arange(n_assign, dtype=jnp.int32) // MOE_TOPK
    buf_tok = jnp.full((n_rows,), n_tok, jnp.int32).at[dest].set(tok_f)
    n_used = (pends[-1] // rb).astype(jnp.int32)
    blk_start = jnp.arange(n_blocks, dtype=jnp.int32) * rb
    blk_start = jnp.minimum(blk_start, pends[-1] - rb)
    blk_eid = jnp.minimum(jnp.searchsorted(pends, blk_start, side="right"), MOE_EXPERTS - 1).astype(jnp.int32)

    h_pad = jnp.concatenate([h, jnp.zeros((1, D_MODEL), h.dtype)], axis=0)
    xb = h_pad[buf_tok]
    w_gu = jnp.concatenate([w_gate, w_up], axis=-1).astype(BF16)
    yb = _experts(xb, blk_eid, n_used.reshape(1), w_gu, w_down.astype(BF16))
    dest2 = dest.reshape(n_tok, MOE_TOPK)
    return _combine(x2, yb[dest2[:, 0]], yb[dest2[:, 1]], wts, final_norm_w)


def _lru_proj_kernel(x_ref, nw_ref, w_ref, xb_ref, gate_ref):
    hb = _rmsnorm(x_ref[...], nw_ref[...]).astype(BF16)
    xb_ref[...] = _dot(hb, w_ref[:, :LRU_WIDTH])
    gate_ref[...] = jax.nn.gelu(_dot(hb, w_ref[:, LRU_WIDTH:]), approximate=True).astype(gate_ref.dtype)


def _lru_proj(x2, norm_w, w_in):
    n_tok = x2.shape[0]
    tm = ROW_TILE
    row = lambda i: (i, 0)
    fixed = lambda i: (0, 0)
    return pl.pallas_call(
        _lru_proj_kernel,
        grid=(n_tok // tm,),
        in_specs=[
            pl.BlockSpec((tm, D_MODEL), row),
            pl.BlockSpec((1, D_MODEL), fixed),
            pl.BlockSpec((D_MODEL, 2 * LRU_WIDTH), fixed),
        ],
        out_specs=[pl.BlockSpec((tm, LRU_WIDTH), row), pl.BlockSpec((tm, LRU_WIDTH), row)],
        out_shape=[
            jax.ShapeDtypeStruct((n_tok, LRU_WIDTH), F32),
            jax.ShapeDtypeStruct((n_tok, LRU_WIDTH), F32),
        ],
        compiler_params=_params("parallel"),
        name="lru_proj",
    )(x2, norm_w.reshape(1, D_MODEL), w_in.astype(BF16))


def _shift_rows(x, d, fill, row):
    return jnp.where(row >= d, pltpu.roll(x, d, axis=0), fill)


def _lru_scan_kernel(xb_ref, gate_ref, res_ref, cw_ref, cb_ref, wr_ref, br_ref, wi_ref, bi_ref,
                     lam_ref, wo_ref, o_ref, xtail_ref, h_ref):
    @pl.when(pl.program_id(1) == 0)
    def _():
        xtail_ref[...] = jnp.zeros_like(xtail_ref)
        h_ref[...] = jnp.zeros_like(h_ref)

    ts = xb_ref.shape[0]
    xb = xb_ref[...]
    x_ext = jnp.concatenate([xtail_ref[...], xb], axis=0)
    xtail_ref[...] = xb[ts - SUBLANES:, :]
    cw = cw_ref[...]
    xc = cb_ref[...] + xb * cw[LRU_CONV_W - 1:LRU_CONV_W, :]
    for back in range(1, LRU_CONV_W):
        tap = LRU_CONV_W - 1 - back
        xc = xc + pltpu.roll(x_ext, back, axis=0)[SUBLANES:, :] * cw[tap:tap + 1, :]

    xcb = xc.astype(BF16)
    r_parts, i_parts = [], []
    for n in range(LRU_BLOCKS):
        blk = xcb[:, n * LRU_BLOCK_W:(n + 1) * LRU_BLOCK_W]
        r_parts.append(_dot(blk, wr_ref[n]))
        i_parts.append(_dot(blk, wi_ref[n]))
    r = jax.nn.sigmoid(jnp.concatenate(r_parts, axis=1) + br_ref[...])
    ig = jax.nn.sigmoid(jnp.concatenate(i_parts, axis=1) + bi_ref[...])

    neg_lam = -lam_ref[...]
    softplus = jnp.maximum(neg_lam, 0.0) + jnp.log1p(jnp.exp(-jnp.abs(neg_lam)))
    log_a = (-LRU_C) * r * softplus
    a = jnp.exp(log_a)
    mult = jnp.sqrt(1.0 - a * a)
    xn = xc * ig * mult

    row = lax.broadcasted_iota(jnp.int32, (ts, 1), 0)
    acc_a, acc_b = a, xn
    d = 1
    while d < ts:
        acc_b = acc_a * _shift_rows(acc_b, d, 0.0, row) + acc_b
        acc_a = acc_a * _shift_rows(acc_a, d, 1.0, row)
        d *= 2
    hs = acc_a * h_ref[...] + acc_b
    h_ref[...] = hs[ts - 1:ts, :]

    y = (hs * gate_ref[...]).astype(BF16)
    o_ref[...] = res_ref[...] + _dot(y, wo_ref[...])


def _lru_scan(xb, gate, res, conv_w, conv_b, w_rgate, b_rgate, w_igate, b_igate, lam, w_out, bsz, seq):
    ts = LRU_SEQ_TILE
    nt = seq // ts
    w = LRU_WIDTH
    row = lambda b, t: (b * nt + t, 0)
    fixed2 = lambda b, t: (0, 0)
    fixed3 = lambda b, t: (0, 0, 0)
    return pl.pallas_call(
        _lru_scan_kernel,
        grid=(bsz, nt),
        in_specs=[
            pl.BlockSpec((ts, w), row),
            pl.BlockSpec((ts, w), row),
            pl.BlockSpec((ts, D_MODEL), row),
            pl.BlockSpec((LRU_CONV_W, w), fixed2),
            pl.BlockSpec((1, w), fixed2),
            pl.BlockSpec((LRU_BLOCKS, LRU_BLOCK_W, LRU_BLOCK_W), fixed3),
            pl.BlockSpec((1, w), fixed2),
            pl.BlockSpec((LRU_BLOCKS, LRU_BLOCK_W, LRU_BLOCK_W), fixed3),
            pl.BlockSpec((1, w), fixed2),
            pl.BlockSpec((1, w), fixed2),
            pl.BlockSpec((w, D_MODEL), fixed2),
        ],
        out_specs=pl.BlockSpec((ts, D_MODEL), row),
        out_shape=jax.ShapeDtypeStruct((bsz * seq, D_MODEL), F32),
        scratch_shapes=[pltpu.VMEM((SUBLANES, w), F32), pltpu.VMEM((1, w), F32)],
        compiler_params=_params("parallel", "arbitrary"),
        name="lru_scan",
    )(xb, gate, res, conv_w, conv_b.reshape(1, w), w_rgate.astype(BF16), b_rgate.reshape(1, w),
      w_igate.astype(BF16), b_igate.reshape(1, w), lam.reshape(1, w), w_out.astype(BF16))


def kernel(x, gla_norm, gla_w_in, gla_w_alpha, gla_b_alpha, gla_head_norm, gla_w_out, lru_norm, lru_w_in, lru_conv_w, lru_conv_b, lru_w_rgate, lru_b_rgate, lru_w_igate, lru_b_igate, lru_lambda, lru_w_out, moe_norm, moe_w_group, moe_w_expert, moe_w_gate, moe_w_up, moe_w_down, final_norm):
    bsz, seq, d = x.shape
    x2 = x.reshape(bsz * seq, d)

    q, k, v, g, la = _gla_proj(x2, gla_norm[0], gla_w_in[0], gla_w_alpha[0], gla_b_alpha[0])
    y = _gla_chunk(q, k, v, g, la, gla_head_norm[0], bsz, seq)
    x2 = _matmul_residual(y, gla_w_out[0], x2)
    x2 = _hier_moe(x2, moe_norm[0], moe_w_group[0], moe_w_expert[0],
                   moe_w_gate[0], moe_w_up[0], moe_w_down[0])

    xb, gate = _lru_proj(x2, lru_norm[0], lru_w_in[0])
    x2 = _lru_scan(xb, gate, x2, lru_conv_w[0], lru_conv_b[0], lru_w_rgate[0], lru_b_rgate[0],
                   lru_w_igate[0], lru_b_igate[0], lru_lambda[0], lru_w_out[0], bsz, seq)
    x2 = _hier_moe(x2, moe_norm[1], moe_w_group[1], moe_w_expert[1],
                   moe_w_gate[1], moe_w_up[1], moe_w_down[1], final_norm)
    return x2.reshape(bsz, seq, d)
```

```python
import functools

import jax
import jax.numpy as jnp
from jax import lax
from jax.experimental import pallas as pl
from jax.experimental.pallas import tpu as pltpu

F32 = jnp.float32
BF16 = jnp.bfloat16

RMS_EPS = 1e-6
D_MODEL = 1024

GLA_HEADS = 4
GLA_DK = 512
GLA_DV = 1024
GLA_DK_HEAD = 128
GLA_DV_HEAD = 256
GLA_GATE_RANK = 16
GLA_GATE_TEMP = 16.0
GLA_CHUNK = 64

LRU_WIDTH = 1024
LRU_BLOCKS = 4
LRU_BLOCK_W = 256
LRU_CONV_W = 4
LRU_C = 8.0

MOE_GROUPS = 4
MOE_EXPERTS_PER_GROUP = 8
MOE_EXPERTS = 32
MOE_TOPK = 2
MOE_FF = 256

LANES = 128
SUBLANES = 8
VMEM_LIMIT = 48 * 1024 * 1024

ROW_TILE = 512
GLA_SEQ_TILE = 512
LRU_SEQ_TILE = 256
MOE_ROW_BLOCK = 256


def _params(*sem):
    return pltpu.CompilerParams(dimension_semantics=sem, vmem_limit_bytes=VMEM_LIMIT)


def _rmsnorm(x, w):
    return x * lax.rsqrt(jnp.mean(x * x, axis=-1, keepdims=True) + RMS_EPS) * w


def _dot(a, b):
    return jnp.dot(a, b, preferred_element_type=F32)


def _split3(x):
    hi = x.astype(BF16)
    r1 = x - hi.astype(F32)
    mid = r1.astype(BF16)
    lo = (r1 - mid.astype(F32)).astype(BF16)
    return hi, mid, lo


def _gla_proj_kernel(x_ref, nw_ref, w_ref, wlr_ref, wal_ref, bal_ref,
                     q_ref, k_ref, v_ref, g_ref, la_ref):
    hb = _rmsnorm(x_ref[...], nw_ref[...]).astype(BF16)
    q_ref[...] = _dot(hb, w_ref[:, 0:GLA_DK])
    k_ref[...] = _dot(hb, w_ref[:, GLA_DK:2 * GLA_DK])
    v_ref[...] = _dot(hb, w_ref[:, 2 * GLA_DK:2 * GLA_DK + GLA_DV]).astype(BF16)
    g_ref[...] = _dot(hb, w_ref[:, 2 * GLA_DK + GLA_DV:2 * GLA_DK + 2 * GLA_DV])
    lr = _dot(hb, wlr_ref[...])
    z = _dot(lr.astype(BF16), wal_ref[...]) + bal_ref[...]
    la_ref[...] = (jnp.minimum(z, 0.0) - jnp.log1p(jnp.exp(-jnp.abs(z)))) / GLA_GATE_TEMP


def _gla_proj(x2, norm_w, w_in, w_alpha, b_alpha):
    n_tok = x2.shape[0]
    n_main = 2 * GLA_DK + 2 * GLA_DV
    w_main = w_in[:, :n_main].astype(BF16)
    w_lr = jnp.pad(w_in[:, n_main:], ((0, 0), (0, LANES - GLA_GATE_RANK))).astype(BF16)
    w_al = jnp.pad(w_alpha, ((0, LANES - GLA_GATE_RANK), (0, 0))).astype(BF16)
    tm = ROW_TILE
    row = lambda i: (i, 0)
    fixed = lambda i: (0, 0)
    return pl.pallas_call(
        _gla_proj_kernel,
        grid=(n_tok // tm,),
        in_specs=[
            pl.BlockSpec((tm, D_MODEL), row),
            pl.BlockSpec((1, D_MODEL), fixed),
            pl.BlockSpec((D_MODEL, n_main), fixed),
            pl.BlockSpec((D_MODEL, LANES), fixed),
            pl.BlockSpec((LANES, GLA_DK), fixed),
            pl.BlockSpec((1, GLA_DK), fixed),
        ],
        out_specs=[
            pl.BlockSpec((tm, GLA_DK), row),
            pl.BlockSpec((tm, GLA_DK), row),
            pl.BlockSpec((tm, GLA_DV), row),
            pl.BlockSpec((tm, GLA_DV), row),
            pl.BlockSpec((tm, GLA_DK), row),
        ],
        out_shape=[
            jax.ShapeDtypeStruct((n_tok, GLA_DK), F32),
            jax.ShapeDtypeStruct((n_tok, GLA_DK), F32),
            jax.ShapeDtypeStruct((n_tok, GLA_DV), BF16),
            jax.ShapeDtypeStruct((n_tok, GLA_DV), F32),
            jax.ShapeDtypeStruct((n_tok, GLA_DK), F32),
        ],
        compiler_params=_params("parallel"),
        name="gla_proj",
    )(x2, norm_w.reshape(1, D_MODEL), w_main, w_lr, w_al, b_alpha.reshape(1, GLA_DK))


def _gla_chunk_kernel(q_ref, k_ref, v_ref, g_ref, la_ref, hnw_ref, y_ref, state_ref):
    @pl.when(pl.program_id(2) == 0)
    def _():
        state_ref[...] = jnp.zeros_like(state_ref)

    c = GLA_CHUNK
    rows = lax.broadcasted_iota(jnp.int32, (c, c), 0)
    cols = lax.broadcasted_iota(jnp.int32, (c, c), 1)
    causal = cols <= rows
    tri = jnp.where(causal, 1.0, 0.0).astype(BF16)
    scale = GLA_DK_HEAD ** -0.5
    hnw = hnw_ref[...]
    state_t = state_ref[...]

    for ci in range(q_ref.shape[0] // c):
        sl = pl.ds(ci * c, c)
        q = q_ref[sl, :] * scale
        k = k_ref[sl, :]
        v = v_ref[sl, :]
        la_hi, la_mid, la_lo = _split3(la_ref[sl, :])
        cum = _dot(tri, la_hi) + _dot(tri, la_mid) + _dot(tri, la_lo)
        mid = cum[c // 2:c // 2 + 1, :]
        last = cum[c - 1:c, :]
        q_in = (q * jnp.exp(cum - mid)).astype(BF16)
        k_in = (k * jnp.exp(mid - cum)).astype(BF16)
        scores = lax.dot_general(q_in, k_in, (((1,), (1,)), ((), ())), preferred_element_type=F32)
        scores = jnp.where(causal, scores, 0.0).astype(BF16)
        o = _dot(scores, v)
        q_dec = (q * jnp.exp(cum)).astype(BF16)
        o = o + lax.dot_general(q_dec, state_t.astype(BF16), (((1,), (1,)), ((), ())),
                                preferred_element_type=F32)
        k_st = (k * jnp.exp(last - cum)).astype(BF16)
        u_t = lax.dot_general(v, k_st, (((0,), (0,)), ((), ())), preferred_element_type=F32)
        state_t = state_t * jnp.exp(last) + u_t
        g = g_ref[sl, :]
        y = _rmsnorm(o, hnw) * (g * jax.nn.sigmoid(g))
        y_ref[sl, :] = y.astype(y_ref.dtype)

    state_ref[...] = state_t


def _gla_chunk(q, k, v, g, la, head_norm_w, bsz, seq):
    ts = GLA_SEQ_TILE
    nt = seq // ts
    blk = lambda b, h, t: (b * nt + t, h)
    return pl.pallas_call(
        _gla_chunk_kernel,
        grid=(bsz, GLA_HEADS, nt),
        in_specs=[
            pl.BlockSpec((ts, GLA_DK_HEAD), blk),
            pl.BlockSpec((ts, GLA_DK_HEAD), blk),
            pl.BlockSpec((ts, GLA_DV_HEAD), blk),
            pl.BlockSpec((ts, GLA_DV_HEAD), blk),
            pl.BlockSpec((ts, GLA_DK_HEAD), blk),
            pl.BlockSpec((1, GLA_DV_HEAD), lambda b, h, t: (0, 0)),
        ],
        out_specs=pl.BlockSpec((ts, GLA_DV_HEAD), blk),
        out_shape=jax.ShapeDtypeStruct((bsz * seq, GLA_DV), BF16),
        scratch_shapes=[pltpu.VMEM((GLA_DV_HEAD, GLA_DK_HEAD), F32)],
        compiler_params=_params("parallel", "parallel", "arbitrary"),
        name="gla_chunk",
    )(q, k, v, g, la, head_norm_w.reshape(1, GLA_DV_HEAD))


def _matmul_residual_kernel(y_ref, w_ref, res_ref, o_ref):
    o_ref[...] = res_ref[...] + _dot(y_ref[...], w_ref[...])


def _matmul_residual(y, w, res):
    n_tok, kdim = y.shape
    tm = ROW_TILE
    row = lambda i: (i, 0)
    return pl.pallas_call(
        _matmul_residual_kernel,
        grid=(n_tok // tm,),
        in_specs=[
            pl.BlockSpec((tm, kdim), row),
            pl.BlockSpec((kdim, D_MODEL), lambda i: (0, 0)),
            pl.BlockSpec((tm, D_MODEL), row),
        ],
        out_specs=pl.BlockSpec((tm, D_MODEL), row),
        out_shape=jax.ShapeDtypeStruct((n_tok, D_MODEL), F32),
        compiler_params=_params("parallel"),
        name="matmul_residual",
    )(y, w.astype(BF16), res)


def _pack_bf16_pairs(x):
    n = x.shape[1] // 2
    lo = pltpu.bitcast(x[:, :n].astype(BF16).astype(F32), jnp.uint32) >> 16
    hi = pltpu.bitcast(x[:, n:].astype(BF16).astype(F32), jnp.uint32) & jnp.uint32(0xFFFF0000)
    return lo | hi


def _unpack_bf16_pairs(p):
    lo = pltpu.bitcast(p << 16, F32)
    hi = pltpu.bitcast(p & jnp.uint32(0xFFFF0000), F32)
    return jnp.concatenate([lo, hi], axis=1)


def _router_kernel(x_ref, nw_ref, whi_ref, wlo_ref, hp_ref, ids_ref, wts_ref, cnt_ref, run_ref):
    @pl.when(pl.program_id(0) == 0)
    def _():
        run_ref[...] = jnp.zeros_like(run_ref)

    h = _rmsnorm(x_ref[...], nw_ref[...])
    h_hi = h.astype(BF16)
    hp_ref[...] = _pack_bf16_pairs(h)
    h_lo = (h - h_hi.astype(F32)).astype(BF16)
    logits = _dot(h_hi, whi_ref[...]) + _dot(h_lo, whi_ref[...]) + _dot(h_hi, wlo_ref[...])
    col = lax.broadcasted_iota(jnp.int32, logits.shape, 1)
    neg = -jnp.inf

    gl = jnp.where(col < MOE_GROUPS, logits, neg)
    gmax = jnp.max(gl, axis=-1, keepdims=True)
    g_sel = jnp.min(jnp.where(gl == gmax, col, LANES), axis=-1, keepdims=True)
    p_g = 1.0 / jnp.sum(jnp.exp(gl - gmax), axis=-1, keepdims=True)

    lo = MOE_GROUPS + MOE_EXPERTS_PER_GROUP * g_sel
    el = jnp.where((col >= lo) & (col < lo + MOE_EXPERTS_PER_GROUP), logits, neg)
    m1 = jnp.max(el, axis=-1, keepdims=True)
    i1 = jnp.min(jnp.where(el == m1, col, LANES), axis=-1, keepdims=True)
    el2 = jnp.where(col == i1, neg, el)
    m2 = jnp.max(el2, axis=-1, keepdims=True)
    i2 = jnp.min(jnp.where(el2 == m2, col, LANES), axis=-1, keepdims=True)
    e2 = jnp.exp(m2 - m1)
    w1 = p_g / (1.0 + e2)
    w2 = p_g * e2 / (1.0 + e2)

    e1 = i1 - MOE_GROUPS
    e2 = i2 - MOE_GROUPS

    tm = logits.shape[0]
    hit1 = col == e1
    hit2 = col == e2
    member = jnp.where(hit1 | hit2, 1.0, 0.0)
    r_i = lax.broadcasted_iota(jnp.int32, (tm, tm), 0)
    c_i = lax.broadcasted_iota(jnp.int32, (tm, tm), 1)
    strict_lower = jnp.where(c_i < r_i, 1.0, 0.0).astype(BF16)
    before = _dot(strict_lower, member.astype(BF16)) + run_ref[...]
    rank1 = jnp.sum(jnp.where(hit1, before, 0.0), axis=-1, keepdims=True).astype(jnp.int32)
    rank2 = jnp.sum(jnp.where(hit2, before, 0.0), axis=-1, keepdims=True).astype(jnp.int32)
    run_ref[...] = run_ref[...] + jnp.sum(member, axis=0, keepdims=True)
    cnt_ref[...] = jnp.broadcast_to(run_ref[...], cnt_ref.shape)

    ids_ref[...] = jnp.where(col == 0, e1, jnp.where(col == 1, e2,
                             jnp.where(col == 2, rank1, jnp.where(col == 3, rank2, 0))))
    wts_ref[...] = jnp.where(col == 0, w1, jnp.where(col == 1, w2, 0.0))


def _router(x2, norm_w, w_group, w_expert):
    n_tok = x2.shape[0]
    w_r = jnp.concatenate([w_group, w_expert], axis=1)
    w_r = jnp.pad(w_r, ((0, 0), (0, LANES - w_r.shape[1])))
    w_hi = w_r.astype(BF16)
    w_lo = (w_r - w_hi.astype(F32)).astype(BF16)
    tm = ROW_TILE
    row = lambda i: (i, 0)
    fixed = lambda i: (0, 0)
    return pl.pallas_call(
        _router_kernel,
        grid=(n_tok // tm,),
        in_specs=[
            pl.BlockSpec((tm, D_MODEL), row),
            pl.BlockSpec((1, D_MODEL), fixed),
            pl.BlockSpec((D_MODEL, LANES), fixed),
            pl.BlockSpec((D_MODEL, LANES), fixed),
        ],
        out_specs=[
            pl.BlockSpec((tm, D_MODEL // 2), row),
            pl.BlockSpec((tm, LANES), row),
            pl.BlockSpec((tm, LANES), row),
            pl.BlockSpec((SUBLANES, LANES), fixed),
        ],
        out_shape=[
            jax.ShapeDtypeStruct((n_tok, D_MODEL // 2), jnp.uint32),
            jax.ShapeDtypeStruct((n_tok, LANES), jnp.int32),
            jax.ShapeDtypeStruct((n_tok, LANES), F32),
            jax.ShapeDtypeStruct((SUBLANES, LANES), F32),
        ],
        scratch_shapes=[pltpu.VMEM((1, LANES), F32)],
        compiler_params=_params("arbitrary"),
        name="moe_router",
    )(x2, norm_w.reshape(1, D_MODEL), w_hi, w_lo)


def _dispatch_kernel(dest_ref, hp_ref, xb_in_ref, xb_ref, sem):
    del xb_in_ref
    tm = hp_ref.shape[0]

    def row_copy(r, d):
        return pltpu.make_async_copy(hp_ref.at[pl.ds(r, 1)], xb_ref.at[pl.ds(d, 1)], sem)

    def body(r, carry):
        for k in range(MOE_TOPK):
            row_copy(r, dest_ref[MOE_TOPK * r + k]).start()
        return carry

    lax.fori_loop(0, tm, body, 0, unroll=8)

    def drain(r, carry):
        for k in range(MOE_TOPK):
            row_copy(r, dest_ref[MOE_TOPK * r + k]).wait()
        return carry

    lax.fori_loop(0, tm, drain, 0, unroll=8)


def _dispatch(hp, dest_flat, n_rows):
    n_tok, width = hp.shape
    tm = ROW_TILE
    return pl.pallas_call(
        _dispatch_kernel,
        grid=(n_tok // tm,),
        in_specs=[
            pl.BlockSpec((MOE_TOPK * tm,), lambda i: (i,), memory_space=pltpu.SMEM),
            pl.BlockSpec((tm, width), lambda i: (i, 0)),
            pl.BlockSpec(memory_space=pl.ANY),
        ],
        out_specs=pl.BlockSpec(memory_space=pl.ANY),
        out_shape=jax.ShapeDtypeStruct((n_rows, width), hp.dtype),
        scratch_shapes=[pltpu.SemaphoreType.DMA(())],
        input_output_aliases={2: 0},
        compiler_params=_params("arbitrary"),
        name="moe_dispatch",
    )(dest_flat, hp, jnp.zeros((n_rows, width), hp.dtype))


def _expert_kernel(eid_ref, nused_ref, x_ref, wgu_ref, wd_ref, y_ref):
    i = pl.program_id(0)

    @pl.when(i < nused_ref[0])
    def _():
        x = _unpack_bf16_pairs(x_ref[...]).astype(BF16)
        gu = _dot(x, wgu_ref[...])
        gate = gu[:, :MOE_FF]
        act = (gate * jax.nn.sigmoid(gate)) * gu[:, MOE_FF:]
        y_ref[...] = _pack_bf16_pairs(_dot(act.astype(BF16), wd_ref[...]))

    @pl.when(i >= nused_ref[0])
    def _():
        y_ref[...] = jnp.zeros_like(y_ref)


def _experts(xb, blk_eid, n_used, w_gu, w_d):
    n_rows, width = xb.shape
    rb = MOE_ROW_BLOCK
    x_map = lambda i, eid, nu: (jnp.minimum(i, nu[0] - 1), 0)
    grid_spec = pltpu.PrefetchScalarGridSpec(
        num_scalar_prefetch=2,
        grid=(n_rows // rb,),
        in_specs=[
            pl.BlockSpec((rb, width), x_map),
            pl.BlockSpec((None, D_MODEL, 2 * MOE_FF), lambda i, eid, nu: (eid[i], 0, 0)),
            pl.BlockSpec((None, MOE_FF, D_MODEL), lambda i, eid, nu: (eid[i], 0, 0)),
        ],
        out_specs=pl.BlockSpec((rb, width), lambda i, eid, nu: (i, 0)),
    )
    return pl.pallas_call(
        _expert_kernel,
        grid_spec=grid_spec,
        out_shape=jax.ShapeDtypeStruct((n_rows, width), jnp.uint32),
        compiler_params=_params("arbitrary"),
        name="moe_experts",
    )(blk_eid, n_used, xb, w_gu, w_d)


def _combine_kernel(dcur_ref, dnext_ref, x_ref, wts_ref, nw_ref, yb_ref, o_ref, gbuf, sems, *, final_norm):
    i = pl.program_id(0)
    n = pl.num_programs(0)
    tm = x_ref.shape[0]

    def row_copy(dref, slot, r, k):
        return pltpu.make_async_copy(yb_ref.at[pl.ds(dref[MOE_TOPK * r + k], 1)],
                                     gbuf.at[slot, k, pl.ds(r, 1)], sems.at[slot])

    def issue(dref, slot):
        def body(r, carry):
            for k in range(MOE_TOPK):
                row_copy(dref, slot, r, k).start()
            return carry
        lax.fori_loop(0, tm, body, 0, unroll=8)

    @pl.when(i == 0)
    def _():
        issue(dcur_ref, 0)

    @pl.when(i + 1 < n)
    def _():
        issue(dnext_ref, (i + 1) % 2)

    slot = i % 2

    def drain(r, carry):
        for k in range(MOE_TOPK):
            row_copy(dcur_ref, slot, r, k).wait()
        return carry

    lax.fori_loop(0, tm, drain, 0, unroll=8)

    w = wts_ref[...]
    out = (x_ref[...] + w[:, 0:1] * _unpack_bf16_pairs(gbuf[slot, 0])
           + w[:, 1:2] * _unpack_bf16_pairs(gbuf[slot, 1]))
    if final_norm:
        out = _rmsnorm(out, nw_ref[...])
    o_ref[...] = out


def _combine(x2, yb, dest_flat, wts, norm_w, final_norm):
    n_tok = x2.shape[0]
    width = yb.shape[1]
    tm = ROW_TILE
    n_tiles = n_tok // tm
    row = lambda i: (i, 0)
    return pl.pallas_call(
        functools.partial(_combine_kernel, final_norm=final_norm),
        grid=(n_tiles,),
        in_specs=[
            pl.BlockSpec((MOE_TOPK * tm,), lambda i: (i,), memory_space=pltpu.SMEM),
            pl.BlockSpec((MOE_TOPK * tm,), lambda i: (jnp.minimum(i + 1, n_tiles - 1),),
                         memory_space=pltpu.SMEM),
            pl.BlockSpec((tm, D_MODEL), row),
            pl.BlockSpec((tm, LANES), row),
            pl.BlockSpec((1, D_MODEL), lambda i: (0, 0)),
            pl.BlockSpec(memory_space=pl.ANY),
        ],
        out_specs=pl.BlockSpec((tm, D_MODEL), row),
        out_shape=jax.ShapeDtypeStruct((n_tok, D_MODEL), F32),
        scratch_shapes=[pltpu.VMEM((2, MOE_TOPK, tm, width), jnp.uint32),
                        pltpu.SemaphoreType.DMA((2,))],
        compiler_params=_params("arbitrary"),
        name="moe_combine",
    )(dest_flat, dest_flat, x2, wts, norm_w.reshape(1, D_MODEL), yb)


def _hier_moe(x2, norm_w, w_group, w_expert, w_gate, w_up, w_down, out_norm_w, final_norm):
    n_tok = x2.shape[0]
    rb = MOE_ROW_BLOCK
    hp, ids, wts, cnt = _router(x2, norm_w, w_group, w_expert)

    counts = cnt[0, :MOE_EXPERTS].astype(jnp.int32)
    pcounts = ((counts + rb - 1) // rb) * rb
    pends = jnp.cumsum(pcounts)
    pstarts = pends - pcounts
    experts = jnp.arange(MOE_EXPERTS, dtype=jnp.int32)
    eid = ids[:, :MOE_TOPK]
    rank = ids[:, MOE_TOPK:2 * MOE_TOPK]
    start_of = jnp.sum(jnp.where(eid[:, :, None] == experts, pstarts, 0), axis=-1)
    dest_flat = (start_of + rank).reshape(-1)
    n_blocks = (n_tok * MOE_TOPK) // rb + MOE_EXPERTS
    n_used = (pends[-1] // rb).astype(jnp.int32)
    blk_start = jnp.minimum(jnp.arange(n_blocks, dtype=jnp.int32) * rb, pends[-1] - rb)
    blk_eid = jnp.sum((blk_start[:, None] >= pends[None, :]).astype(jnp.int32), axis=-1)

    xb = _dispatch(hp, dest_flat, n_blocks * rb)
    w_gu = jnp.concatenate([w_gate, w_up], axis=-1).astype(BF16)
    yb = _experts(xb, blk_eid, n_used.reshape(1), w_gu, w_down.astype(BF16))
    return _combine(x2, yb, dest_flat, wts, out_norm_w, final_norm)


def _lru_proj_kernel(x_ref, nw_ref, w_ref, xb_ref, gate_ref):
    hb = _rmsnorm(x_ref[...], nw_ref[...]).astype(BF16)
    xb_ref[...] = _dot(hb, w_ref[:, :LRU_WIDTH])
    gate_ref[...] = jax.nn.gelu(_dot(hb, w_ref[:, LRU_WIDTH:]), approximate=True).astype(gate_ref.dtype)


def _lru_proj(x2, norm_w, w_in):
    n_tok = x2.shape[0]
    tm = ROW_TILE
    row = lambda i: (i, 0)
    fixed = lambda i: (0, 0)
    return pl.pallas_call(
        _lru_proj_kernel,
        grid=(n_tok // tm,),
        in_specs=[
            pl.BlockSpec((tm, D_MODEL), row),
            pl.BlockSpec((1, D_MODEL), fixed),
            pl.BlockSpec((D_MODEL, 2 * LRU_WIDTH), fixed),
        ],
        out_specs=[pl.BlockSpec((tm, LRU_WIDTH), row), pl.BlockSpec((tm, LRU_WIDTH), row)],
        out_shape=[
            jax.ShapeDtypeStruct((n_tok, LRU_WIDTH), F32),
            jax.ShapeDtypeStruct((n_tok, LRU_WIDTH), F32),
        ],
        compiler_params=_params("parallel"),
        name="lru_proj",
    )(x2, norm_w.reshape(1, D_MODEL), w_in.astype(BF16))


def _shift_rows(x, d, fill, row):
    return jnp.where(row >= d, pltpu.roll(x, d, axis=0), fill)


def _lru_scan_kernel(xb_ref, gate_ref, res_ref, cw_ref, cb_ref, wr_ref, br_ref, wi_ref, bi_ref,
                     lam_ref, wo_ref, o_ref, xtail_ref, h_ref):
    @pl.when(pl.program_id(1) == 0)
    def _():
        xtail_ref[...] = jnp.zeros_like(xtail_ref)
        h_ref[...] = jnp.zeros_like(h_ref)

    ts = xb_ref.shape[0]
    xb = xb_ref[...]
    x_ext = jnp.concatenate([xtail_ref[...], xb], axis=0)
    xtail_ref[...] = xb[ts - SUBLANES:, :]
    cw = cw_ref[...]
    xc = cb_ref[...] + xb * cw[LRU_CONV_W - 1:LRU_CONV_W, :]
    for back in range(1, LRU_CONV_W):
        tap = LRU_CONV_W - 1 - back
        xc = xc + pltpu.roll(x_ext, back, axis=0)[SUBLANES:, :] * cw[tap:tap + 1, :]

    xcb = xc.astype(BF16)
    r_parts, i_parts = [], []
    for n in range(LRU_BLOCKS):
        blk = xcb[:, n * LRU_BLOCK_W:(n + 1) * LRU_BLOCK_W]
        r_parts.append(_dot(blk, wr_ref[n]))
        i_parts.append(_dot(blk, wi_ref[n]))
    r = jax.nn.sigmoid(jnp.concatenate(r_parts, axis=1) + br_ref[...])
    ig = jax.nn.sigmoid(jnp.concatenate(i_parts, axis=1) + bi_ref[...])

    neg_lam = -lam_ref[...]
    softplus = jnp.maximum(neg_lam, 0.0) + jnp.log1p(jnp.exp(-jnp.abs(neg_lam)))
    log_a = (-LRU_C) * r * softplus
    a = jnp.exp(log_a)
    mult = jnp.sqrt(1.0 - a * a)
    xn = xc * ig * mult

    row = lax.broadcasted_iota(jnp.int32, (ts, 1), 0)
    acc_a, acc_b = a, xn
    d = 1
    while d < ts:
        acc_b = acc_a * _shift_rows(acc_b, d, 0.0, row) + acc_b
        acc_a = acc_a * _shift_rows(acc_a, d, 1.0, row)
        d *= 2
    hs = acc_a * h_ref[...] + acc_b
    h_ref[...] = hs[ts - 1:ts, :]

    y = (hs * gate_ref[...]).astype(BF16)
    o_ref[...] = res_ref[...] + _dot(y, wo_ref[...])


def _lru_scan(xb, gate, res, conv_w, conv_b, w_rgate, b_rgate, w_igate, b_igate, lam, w_out, bsz, seq):
    ts = LRU_SEQ_TILE
    nt = seq // ts
    w = LRU_WIDTH
    row = lambda b, t: (b * nt + t, 0)
    fixed2 = lambda b, t: (0, 0)
    fixed3 = lambda b, t: (0, 0, 0)
    return pl.pallas_call(
        _lru_scan_kernel,
        grid=(bsz, nt),
        in_specs=[
            pl.BlockSpec((ts, w), row),
            pl.BlockSpec((ts, w), row),
            pl.BlockSpec((ts, D_MODEL), row),
            pl.BlockSpec((LRU_CONV_W, w), fixed2),
            pl.BlockSpec((1, w), fixed2),
            pl.BlockSpec((LRU_BLOCKS, LRU_BLOCK_W, LRU_BLOCK_W), fixed3),
            pl.BlockSpec((1, w), fixed2),
            pl.BlockSpec((LRU_BLOCKS, LRU_BLOCK_W, LRU_BLOCK_W), fixed3),
            pl.BlockSpec((1, w), fixed2),
            pl.BlockSpec((1, w), fixed2),
            pl.BlockSpec((w, D_MODEL), fixed2),
        ],
        out_specs=pl.BlockSpec((ts, D_MODEL), row),
        out_shape=jax.ShapeDtypeStruct((bsz * seq, D_MODEL), F32),
        scratch_shapes=[pltpu.VMEM((SUBLANES, w), F32), pltpu.VMEM((1, w), F32)],
        compiler_params=_params("parallel", "arbitrary"),
        name="lru_scan",
    )(xb, gate, res, conv_w, conv_b.reshape(1, w), w_rgate.astype(BF16), b_rgate.reshape(1, w),
      w_igate.astype(BF16), b_igate.reshape(1, w), lam.reshape(1, w), w_out.astype(BF16))


def kernel(x, gla_norm, gla_w_in, gla_w_alpha, gla_b_alpha, gla_head_norm, gla_w_out, lru_norm, lru_w_in, lru_conv_w, lru_conv_b, lru_w_rgate, lru_b_rgate, lru_w_igate, lru_b_igate, lru_lambda, lru_w_out, moe_norm, moe_w_group, moe_w_expert, moe_w_gate, moe_w_up, moe_w_down, final_norm):
    bsz, seq, d = x.shape
    x2 = x.reshape(bsz * seq, d)

    q, k, v, g, la = _gla_proj(x2, gla_norm[0], gla_w_in[0], gla_w_alpha[0], gla_b_alpha[0])
    y = _gla_chunk(q, k, v, g, la, gla_head_norm[0], bsz, seq)
    x2 = _matmul_residual(y, gla_w_out[0], x2)
    x2 = _hier_moe(x2, moe_norm[0], moe_w_group[0], moe_w_expert[0],
                   moe_w_gate[0], moe_w_up[0], moe_w_down[0], final_norm, False)

    xb, gate = _lru_proj(x2, lru_norm[0], lru_w_in[0])
    x2 = _lru_scan(xb, gate, x2, lru_conv_w[0], lru_conv_b[0], lru_w_rgate[0], lru_b_rgate[0],
                   lru_w_igate[0], lru_b_igate[0], lru_lambda[0], lru_w_out[0], bsz, seq)
    x2 = _hier_moe(x2, moe_norm[1], moe_w_group[1], moe_w_expert[1],
                   moe_w_gate[1], moe_w_up[1], moe_w_down[1], final_norm, True)
    return x2.reshape(bsz, seq, d)
```

```python
import functools

import jax
import jax.numpy as jnp
from jax import lax
from jax.experimental import pallas as pl
from jax.experimental.pallas import tpu as pltpu

F32 = jnp.float32
BF16 = jnp.bfloat16

RMS_EPS = 1e-6
D_MODEL = 1024

GLA_HEADS = 4
GLA_DK = 512
GLA_DV = 1024
GLA_DK_HEAD = 128
GLA_DV_HEAD = 256
GLA_GATE_RANK = 16
GLA_GATE_TEMP = 16.0
GLA_CHUNK = 64

LRU_WIDTH = 1024
LRU_BLOCKS = 4
LRU_BLOCK_W = 256
LRU_CONV_W = 4
LRU_C = 8.0

MOE_GROUPS = 4
MOE_EXPERTS_PER_GROUP = 8
MOE_EXPERTS = 32
MOE_TOPK = 2
MOE_FF = 256

LANES = 128
SUBLANES = 8
VMEM_LIMIT = 48 * 1024 * 1024

ROW_TILE = 512
GLA_SEQ_TILE = 256
LRU_SEQ_TILE = 256
MOE_ROW_BLOCK = 512


def _params(*sem):
    return pltpu.CompilerParams(dimension_semantics=sem, vmem_limit_bytes=VMEM_LIMIT)


def _rmsnorm(x, w):
    return x * lax.rsqrt(jnp.mean(x * x, axis=-1, keepdims=True) + RMS_EPS) * w


def _dot(a, b):
    return jnp.dot(a, b, preferred_element_type=F32)


def _dot_nt(a, b):
    return lax.dot_general(a, b, (((1,), (1,)), ((), ())), preferred_element_type=F32)


def _dot_tn(a, b):
    return lax.dot_general(a, b, (((0,), (0,)), ((), ())), preferred_element_type=F32)


def _split3(x):
    hi = x.astype(BF16)
    r1 = x - hi.astype(F32)
    mid = r1.astype(BF16)
    lo = (r1 - mid.astype(F32)).astype(BF16)
    return hi, mid, lo


def _gla_mixer_kernel(x_ref, nw_ref, w_ref, wlr_ref, wal_ref, bal_ref, hnw_ref, wo_ref, o_ref,
                      q_s, k_s, v_s, g_s, la_s, y_s, state_ref):
    @pl.when(pl.program_id(1) == 0)
    def _():
        state_ref[...] = jnp.zeros_like(state_ref)

    x = x_ref[...]
    hb = _rmsnorm(x, nw_ref[...]).astype(BF16)
    q_s[...] = _dot(hb, w_ref[:, 0:GLA_DK]) * (GLA_DK_HEAD ** -0.5)
    k_s[...] = _dot(hb, w_ref[:, GLA_DK:2 * GLA_DK])
    v_s[...] = _dot(hb, w_ref[:, 2 * GLA_DK:2 * GLA_DK + GLA_DV]).astype(BF16)
    g_s[...] = _dot(hb, w_ref[:, 2 * GLA_DK + GLA_DV:2 * GLA_DK + 2 * GLA_DV])
    lr = _dot(hb, wlr_ref[...])
    z = _dot(lr.astype(BF16), wal_ref[...]) + bal_ref[...]
    la_s[...] = (jnp.minimum(z, 0.0) - jnp.log1p(jnp.exp(-jnp.abs(z)))) / GLA_GATE_TEMP

    c = GLA_CHUNK
    rows = lax.broadcasted_iota(jnp.int32, (c, c), 0)
    cols = lax.broadcasted_iota(jnp.int32, (c, c), 1)
    causal = cols <= rows
    tri = jnp.where(causal, 1.0, 0.0).astype(BF16)
    hnw = hnw_ref[...]

    for h in range(GLA_HEADS):
        kcol = pl.ds(h * GLA_DK_HEAD, GLA_DK_HEAD)
        vcol = pl.ds(h * GLA_DV_HEAD, GLA_DV_HEAD)
        state_t = state_ref[h]
        for ci in range(x_ref.shape[0] // c):
            sl = pl.ds(ci * c, c)
            q = q_s[sl, kcol]
            k = k_s[sl, kcol]
            v = v_s[sl, vcol]
            la_hi, la_mid, la_lo = _split3(la_s[sl, kcol])
            cum = _dot(tri, la_hi) + _dot(tri, la_mid) + _dot(tri, la_lo)
            mid = cum[c // 2:c // 2 + 1, :]
            last = cum[c - 1:c, :]
            q_in = (q * jnp.exp(cum - mid)).astype(BF16)
            k_in = (k * jnp.exp(mid - cum)).astype(BF16)
            scores = jnp.where(causal, _dot_nt(q_in, k_in), 0.0).astype(BF16)
            q_dec = (q * jnp.exp(cum)).astype(BF16)
            o = _dot(scores, v) + _dot_nt(q_dec, state_t.astype(BF16))
            k_st = (k * jnp.exp(last - cum)).astype(BF16)
            state_t = state_t * jnp.exp(last) + _dot_tn(v, k_st)
            g = g_s[sl, vcol]
            y_s[sl, vcol] = (_rmsnorm(o, hnw) * (g * jax.nn.sigmoid(g))).astype(BF16)
        state_ref[h] = state_t

    o_ref[...] = x + _dot(y_s[...], wo_ref[...])


def _gla_mixer(x2, norm_w, w_in, w_alpha, b_alpha, head_norm_w, w_out, bsz, seq):
    n_main = 2 * GLA_DK + 2 * GLA_DV
    w_main = w_in[:, :n_main].astype(BF16)
    w_lr = jnp.pad(w_in[:, n_main:], ((0, 0), (0, LANES - GLA_GATE_RANK))).astype(BF16)
    w_al = jnp.pad(w_alpha, ((0, LANES - GLA_GATE_RANK), (0, 0))).astype(BF16)
    ts = GLA_SEQ_TILE
    nt = seq // ts
    row = lambda b, t: (b * nt + t, 0)
    fixed = lambda b, t: (0, 0)
    return pl.pallas_call(
        _gla_mixer_kernel,
        grid=(bsz, nt),
        in_specs=[
            pl.BlockSpec((ts, D_MODEL), row),
            pl.BlockSpec((1, D_MODEL), fixed),
            pl.BlockSpec((D_MODEL, n_main), fixed),
            pl.BlockSpec((D_MODEL, LANES), fixed),
            pl.BlockSpec((LANES, GLA_DK), fixed),
            pl.BlockSpec((1, GLA_DK), fixed),
            pl.BlockSpec((1, GLA_DV_HEAD), fixed),
            pl.BlockSpec((GLA_DV, D_MODEL), fixed),
        ],
        out_specs=pl.BlockSpec((ts, D_MODEL), row),
        out_shape=jax.ShapeDtypeStruct((bsz * seq, D_MODEL), F32),
        scratch_shapes=[
            pltpu.VMEM((ts, GLA_DK), F32),
            pltpu.VMEM((ts, GLA_DK), F32),
            pltpu.VMEM((ts, GLA_DV), BF16),
            pltpu.VMEM((ts, GLA_DV), F32),
            pltpu.VMEM((ts, GLA_DK), F32),
            pltpu.VMEM((ts, GLA_DV), BF16),
            pltpu.VMEM((GLA_HEADS, GLA_DV_HEAD, GLA_DK_HEAD), F32),
        ],
        compiler_params=_params("parallel", "arbitrary"),
        name="gla_mixer",
    )(x2, norm_w.reshape(1, D_MODEL), w_main, w_lr, w_al, b_alpha.reshape(1, GLA_DK),
      head_norm_w.reshape(1, GLA_DV_HEAD), w_out.astype(BF16))


def _pack_bf16_pairs(x):
    n = x.shape[1] // 2
    lo = pltpu.bitcast(x[:, :n].astype(BF16).astype(F32), jnp.uint32) >> 16
    hi = pltpu.bitcast(x[:, n:].astype(BF16).astype(F32), jnp.uint32) & jnp.uint32(0xFFFF0000)
    return lo | hi


def _unpack_bf16_pairs(p):
    lo = pltpu.bitcast(p << 16, F32)
    hi = pltpu.bitcast(p & jnp.uint32(0xFFFF0000), F32)
    return jnp.concatenate([lo, hi], axis=1)


def _router_kernel(x_ref, nw_ref, whi_ref, wlo_ref, hp_ref, ids_ref, wts_ref, cnt_ref, run_ref):
    @pl.when(pl.program_id(0) == 0)
    def _():
        run_ref[...] = jnp.zeros_like(run_ref)

    h = _rmsnorm(x_ref[...], nw_ref[...])
    h_hi = h.astype(BF16)
    hp_ref[...] = _pack_bf16_pairs(h)
    h_lo = (h - h_hi.astype(F32)).astype(BF16)
    logits = _dot(h_hi, whi_ref[...]) + _dot(h_lo, whi_ref[...]) + _dot(h_hi, wlo_ref[...])
    col = lax.broadcasted_iota(jnp.int32, logits.shape, 1)
    neg = -jnp.inf

    gl = jnp.where(col < MOE_GROUPS, logits, neg)
    gmax = jnp.max(gl, axis=-1, keepdims=True)
    g_sel = jnp.min(jnp.where(gl == gmax, col, LANES), axis=-1, keepdims=True)
    p_g = 1.0 / jnp.sum(jnp.exp(gl - gmax), axis=-1, keepdims=True)

    lo = MOE_GROUPS + MOE_EXPERTS_PER_GROUP * g_sel
    el = jnp.where((col >= lo) & (col < lo + MOE_EXPERTS_PER_GROUP), logits, neg)
    m1 = jnp.max(el, axis=-1, keepdims=True)
    i1 = jnp.min(jnp.where(el == m1, col, LANES), axis=-1, keepdims=True)
    el2 = jnp.where(col == i1, neg, el)
    m2 = jnp.max(el2, axis=-1, keepdims=True)
    i2 = jnp.min(jnp.where(el2 == m2, col, LANES), axis=-1, keepdims=True)
    ratio = jnp.exp(m2 - m1)
    w1 = p_g / (1.0 + ratio)
    w2 = p_g * ratio / (1.0 + ratio)

    e1 = i1 - MOE_GROUPS
    e2 = i2 - MOE_GROUPS

    tm = logits.shape[0]
    hit1 = col == e1
    hit2 = col == e2
    member = jnp.where(hit1 | hit2, 1.0, 0.0)
    r_i = lax.broadcasted_iota(jnp.int32, (tm, tm), 0)
    c_i = lax.broadcasted_iota(jnp.int32, (tm, tm), 1)
    strict_lower = jnp.where(c_i < r_i, 1.0, 0.0).astype(BF16)
    before = _dot(strict_lower, member.astype(BF16)) + run_ref[...]
    rank1 = jnp.sum(jnp.where(hit1, before, 0.0), axis=-1, keepdims=True).astype(jnp.int32)
    rank2 = jnp.sum(jnp.where(hit2, before, 0.0), axis=-1, keepdims=True).astype(jnp.int32)
    run_ref[...] = run_ref[...] + jnp.sum(member, axis=0, keepdims=True)
    cnt_ref[...] = jnp.broadcast_to(run_ref[...], cnt_ref.shape)

    ids_ref[...] = jnp.where(col == 0, e1, jnp.where(col == 1, e2,
                             jnp.where(col == 2, rank1, jnp.where(col == 3, rank2, 0))))
    wts_ref[...] = jnp.where(col == 0, w1, jnp.where(col == 1, w2, 0.0))


def _router(x2, norm_w, w_group, w_expert):
    n_tok = x2.shape[0]
    w_r = jnp.concatenate([w_group, w_expert], axis=1)
    w_r = jnp.pad(w_r, ((0, 0), (0, LANES - w_r.shape[1])))
    w_hi = w_r.astype(BF16)
    w_lo = (w_r - w_hi.astype(F32)).astype(BF16)
    tm = ROW_TILE
    row = lambda i: (i, 0)
    fixed = lambda i: (0, 0)
    return pl.pallas_call(
        _router_kernel,
        grid=(n_tok // tm,),
        in_specs=[
            pl.BlockSpec((tm, D_MODEL), row),
            pl.BlockSpec((1, D_MODEL), fixed),
            pl.BlockSpec((D_MODEL, LANES), fixed),
            pl.BlockSpec((D_MODEL, LANES), fixed),
        ],
        out_specs=[
            pl.BlockSpec((tm, D_MODEL // 2), row),
            pl.BlockSpec((tm, LANES), row),
            pl.BlockSpec((tm, LANES), row),
            pl.BlockSpec((SUBLANES, LANES), fixed),
        ],
        out_shape=[
            jax.ShapeDtypeStruct((n_tok, D_MODEL // 2), jnp.uint32),
            jax.ShapeDtypeStruct((n_tok, LANES), jnp.int32),
            jax.ShapeDtypeStruct((n_tok, LANES), F32),
            jax.ShapeDtypeStruct((SUBLANES, LANES), F32),
        ],
        scratch_shapes=[pltpu.VMEM((1, LANES), F32)],
        compiler_params=_params("arbitrary"),
        name="moe_router",
    )(x2, norm_w.reshape(1, D_MODEL), w_hi, w_lo)


def _dispatch_kernel(dest_ref, hp_ref, xb_in_ref, xb_ref, sem):
    del xb_in_ref
    tm = hp_ref.shape[0]

    def row_copy(r, d):
        return pltpu.make_async_copy(hp_ref.at[pl.ds(r, 1)], xb_ref.at[pl.ds(d, 1)], sem)

    def body(r, carry):
        for k in range(MOE_TOPK):
            row_copy(r, dest_ref[MOE_TOPK * r + k]).start()
        return carry

    lax.fori_loop(0, tm, body, 0, unroll=8)

    def drain(r, carry):
        for k in range(MOE_TOPK):
            row_copy(r, dest_ref[MOE_TOPK * r + k]).wait()
        return carry

    lax.fori_loop(0, tm, drain, 0, unroll=8)


def _dispatch(hp, dest_flat, n_rows):
    n_tok, width = hp.shape
    tm = ROW_TILE
    return pl.pallas_call(
        _dispatch_kernel,
        grid=(n_tok // tm,),
        in_specs=[
            pl.BlockSpec((MOE_TOPK * tm,), lambda i: (i,), memory_space=pltpu.SMEM),
            pl.BlockSpec((tm, width), lambda i: (i, 0)),
            pl.BlockSpec(memory_space=pl.ANY),
        ],
        out_specs=pl.BlockSpec(memory_space=pl.ANY),
        out_shape=jax.ShapeDtypeStruct((n_rows, width), hp.dtype),
        scratch_shapes=[pltpu.SemaphoreType.DMA(())],
        input_output_aliases={2: 0},
        compiler_params=_params("arbitrary"),
        name="moe_dispatch",
    )(dest_flat, hp, jnp.zeros((n_rows, width), hp.dtype))


def _expert_kernel(eid_ref, nused_ref, x_ref, wg_ref, wu_ref, wd_ref, y_ref):
    i = pl.program_id(0)

    @pl.when(i < nused_ref[0])
    def _():
        x = _unpack_bf16_pairs(x_ref[...]).astype(BF16)
        gate = _dot(x, wg_ref[...].astype(BF16))
        up = _dot(x, wu_ref[...].astype(BF16))
        act = (gate * jax.nn.sigmoid(gate)) * up
        y_ref[...] = _pack_bf16_pairs(_dot(act.astype(BF16), wd_ref[...].astype(BF16)))

    @pl.when(i >= nused_ref[0])
    def _():
        y_ref[...] = jnp.zeros_like(y_ref)


def _experts(xb, blk_eid, n_used, w_gate, w_up, w_down, layer):
    n_rows, width = xb.shape
    rb = MOE_ROW_BLOCK
    x_map = lambda i, eid, nu: (jnp.minimum(i, nu[0] - 1), 0)
    w_map = lambda i, eid, nu: (layer, eid[i], 0, 0)
    grid_spec = pltpu.PrefetchScalarGridSpec(
        num_scalar_prefetch=2,
        grid=(n_rows // rb,),
        in_specs=[
            pl.BlockSpec((rb, width), x_map),
            pl.BlockSpec((None, None, D_MODEL, MOE_FF), w_map),
            pl.BlockSpec((None, None, D_MODEL, MOE_FF), w_map),
            pl.BlockSpec((None, None, MOE_FF, D_MODEL), w_map),
        ],
        out_specs=pl.BlockSpec((rb, width), lambda i, eid, nu: (i, 0)),
    )
    return pl.pallas_call(
        _expert_kernel,
        grid_spec=grid_spec,
        out_shape=jax.ShapeDtypeStruct((n_rows, width), jnp.uint32),
        compiler_params=_params("arbitrary"),
        name="moe_experts",
    )(blk_eid, n_used, xb, w_gate, w_up, w_down)


def _combine_kernel(dcur_ref, dnext_ref, x_ref, wts_ref, nw_ref, yb_ref, o_ref, gbuf, sems, *, final_norm):
    i = pl.program_id(0)
    n = pl.num_programs(0)
    tm = x_ref.shape[0]

    def row_copy(dref, slot, r, k):
        return pltpu.make_async_copy(yb_ref.at[pl.ds(dref[MOE_TOPK * r + k], 1)],
                                     gbuf.at[slot, k, pl.ds(r, 1)], sems.at[slot])

    def issue(dref, slot):
        def body(r, carry):
            for k in range(MOE_TOPK):
                row_copy(dref, slot, r, k).start()
            return carry
        lax.fori_loop(0, tm, body, 0, unroll=8)

    @pl.when(i == 0)
    def _():
        issue(dcur_ref, 0)

    @pl.when(i + 1 < n)
    def _():
        issue(dnext_ref, (i + 1) % 2)

    slot = i % 2

    def drain(r, carry):
        for k in range(MOE_TOPK):
            row_copy(dcur_ref, slot, r, k).wait()
        return carry

    lax.fori_loop(0, tm, drain, 0, unroll=8)

    w = wts_ref[...]
    out = (x_ref[...] + w[:, 0:1] * _unpack_bf16_pairs(gbuf[slot, 0])
           + w[:, 1:2] * _unpack_bf16_pairs(gbuf[slot, 1]))
    if final_norm:
        out = _rmsnorm(out, nw_ref[...])
    o_ref[...] = out


def _combine(x2, yb, dest_flat, wts, norm_w, final_norm):
    n_tok = x2.shape[0]
    width = yb.shape[1]
    tm = ROW_TILE
    n_tiles = n_tok // tm
    row = lambda i: (i, 0)
    return pl.pallas_call(
        functools.partial(_combine_kernel, final_norm=final_norm),
        grid=(n_tiles,),
        in_specs=[
            pl.BlockSpec((MOE_TOPK * tm,), lambda i: (i,), memory_space=pltpu.SMEM),
            pl.BlockSpec((MOE_TOPK * tm,), lambda i: (jnp.minimum(i + 1, n_tiles - 1),),
                         memory_space=pltpu.SMEM),
            pl.BlockSpec((tm, D_MODEL), row),
            pl.BlockSpec((tm, LANES), row),
            pl.BlockSpec((1, D_MODEL), lambda i: (0, 0)),
            pl.BlockSpec(memory_space=pl.ANY),
        ],
        out_specs=pl.BlockSpec((tm, D_MODEL), row),
        out_shape=jax.ShapeDtypeStruct((n_tok, D_MODEL), F32),
        scratch_shapes=[pltpu.VMEM((2, MOE_TOPK, tm, width), jnp.uint32),
                        pltpu.SemaphoreType.DMA((2,))],
        compiler_params=_params("arbitrary"),
        name="moe_combine",
    )(dest_flat, dest_flat, x2, wts, norm_w.reshape(1, D_MODEL), yb)


def _hier_moe(x2, norm_w, w_group, w_expert, w_gate, w_up, w_down, layer, out_norm_w, final_norm):
    n_tok = x2.shape[0]
    rb = MOE_ROW_BLOCK
    hp, ids, wts, cnt = _router(x2, norm_w, w_group, w_expert)

    counts = cnt[0, :MOE_EXPERTS].astype(jnp.int32)
    pcounts = ((counts + rb - 1) // rb) * rb
    pends = jnp.cumsum(pcounts)
    pstarts = pends - pcounts
    experts = jnp.arange(MOE_EXPERTS, dtype=jnp.int32)
    eid = ids[:, :MOE_TOPK]
    rank = ids[:, MOE_TOPK:2 * MOE_TOPK]
    start_of = jnp.sum(jnp.where(eid[:, :, None] == experts, pstarts, 0), axis=-1)
    dest_flat = (start_of + rank).reshape(-1)
    n_blocks = (n_tok * MOE_TOPK) // rb + MOE_EXPERTS
    n_used = (pends[-1] // rb).astype(jnp.int32)
    blk_start = jnp.minimum(jnp.arange(n_blocks, dtype=jnp.int32) * rb, pends[-1] - rb)
    blk_eid = jnp.sum((blk_start[:, None] >= pends[None, :]).astype(jnp.int32), axis=-1)

    xb = _dispatch(hp, dest_flat, n_blocks * rb)
    yb = _experts(xb, blk_eid, n_used.reshape(1), w_gate, w_up, w_down, layer)
    return _combine(x2, yb, dest_flat, wts, out_norm_w, final_norm)


def _linear_scan(a, b, h0):
    ts, w = a.shape
    groups = ts // SUBLANES
    a3 = a.reshape(groups, SUBLANES, w)
    b3 = b.reshape(groups, SUBLANES, w)
    sub = lax.broadcasted_iota(jnp.int32, (1, SUBLANES, 1), 1)
    d = 1
    while d < SUBLANES:
        keep = sub >= d
        b3 = a3 * jnp.where(keep, pltpu.roll(b3, d, axis=1), 0.0) + b3
        a3 = a3 * jnp.where(keep, pltpu.roll(a3, d, axis=1), 1.0)
        d *= 2
    h = h0
    out = []
    for gi in range(groups):
        hg = a3[gi] * h + b3[gi]
        out.append(hg)
        h = hg[SUBLANES - 1:SUBLANES, :]
    return jnp.concatenate(out, axis=0), h


def _lru_mixer_kernel(x_ref, nw_ref, win_ref, cw_ref, cb_ref, wr_ref, br_ref, wi_ref, bi_ref,
                      lam_ref, wo_ref, o_ref, xtail_ref, h_ref):
    @pl.when(pl.program_id(1) == 0)
    def _():
        xtail_ref[...] = jnp.zeros_like(xtail_ref)
        h_ref[...] = jnp.zeros_like(h_ref)

    ts = x_ref.shape[0]
    x = x_ref[...]
    hb = _rmsnorm(x, nw_ref[...]).astype(BF16)
    xb = _dot(hb, win_ref[:, :LRU_WIDTH])
    gate = jax.nn.gelu(_dot(hb, win_ref[:, LRU_WIDTH:]), approximate=True)

    x_ext = jnp.concatenate([xtail_ref[...], xb], axis=0)
    xtail_ref[...] = xb[ts - SUBLANES:, :]
    cw = cw_ref[...]
    xc = cb_ref[...] + xb * cw[LRU_CONV_W - 1:LRU_CONV_W, :]
    for back in range(1, LRU_CONV_W):
        tap = LRU_CONV_W - 1 - back
        xc = xc + pltpu.roll(x_ext, back, axis=0)[SUBLANES:, :] * cw[tap:tap + 1, :]

    xcb = xc.astype(BF16)
    r_parts, i_parts = [], []
    for n in range(LRU_BLOCKS):
        blk = xcb[:, n * LRU_BLOCK_W:(n + 1) * LRU_BLOCK_W]
        r_parts.append(_dot(blk, wr_ref[n]))
        i_parts.append(_dot(blk, wi_ref[n]))
    r = jax.nn.sigmoid(jnp.concatenate(r_parts, axis=1) + br_ref[...])
    ig = jax.nn.sigmoid(jnp.concatenate(i_parts, axis=1) + bi_ref[...])

    neg_lam = -lam_ref[...]
    softplus = jnp.maximum(neg_lam, 0.0) + jnp.log1p(jnp.exp(-jnp.abs(neg_lam)))
    a = jnp.exp((-LRU_C) * r * softplus)
    mult = jnp.sqrt(1.0 - a * a)
    hs, h_last = _linear_scan(a, xc * ig * mult, h_ref[...])
    h_ref[...] = h_last

    y = (hs * gate).astype(BF16)
    o_ref[...] = x + _dot(y, wo_ref[...])


def _lru_mixer(x2, norm_w, w_in, conv_w, conv_b, w_rgate, b_rgate, w_igate, b_igate, lam, w_out, bsz, seq):
    ts = LRU_SEQ_TILE
    nt = seq // ts
    w = LRU_WIDTH
    row = lambda b, t: (b * nt + t, 0)
    fixed2 = lambda b, t: (0, 0)
    fixed3 = lambda b, t: (0, 0, 0)
    return pl.pallas_call(
        _lru_mixer_kernel,
        grid=(bsz, nt),
        in_specs=[
            pl.BlockSpec((ts, D_MODEL), row),
            pl.BlockSpec((1, D_MODEL), fixed2),
            pl.BlockSpec((D_MODEL, 2 * w), fixed2),
            pl.BlockSpec((LRU_CONV_W, w), fixed2),
            pl.BlockSpec((1, w), fixed2),
            pl.BlockSpec((LRU_BLOCKS, LRU_BLOCK_W, LRU_BLOCK_W), fixed3),
            pl.BlockSpec((1, w), fixed2),
            pl.BlockSpec((LRU_BLOCKS, LRU_BLOCK_W, LRU_BLOCK_W), fixed3),
            pl.BlockSpec((1, w), fixed2),
            pl.BlockSpec((1, w), fixed2),
            pl.BlockSpec((w, D_MODEL), fixed2),
        ],
        out_specs=pl.BlockSpec((ts, D_MODEL), row),
        out_shape=jax.ShapeDtypeStruct((bsz * seq, D_MODEL), F32),
        scratch_shapes=[pltpu.VMEM((SUBLANES, w), F32), pltpu.VMEM((1, w), F32)],
        compiler_params=_params("parallel", "arbitrary"),
        name="lru_mixer",
    )(x2, norm_w.reshape(1, D_MODEL), w_in.astype(BF16), conv_w, conv_b.reshape(1, w),
      w_rgate.astype(BF16), b_rgate.reshape(1, w), w_igate.astype(BF16), b_igate.reshape(1, w),
      lam.reshape(1, w), w_out.astype(BF16))


def kernel(x, gla_norm, gla_w_in, gla_w_alpha, gla_b_alpha, gla_head_norm, gla_w_out, lru_norm, lru_w_in, lru_conv_w, lru_conv_b, lru_w_rgate, lru_b_rgate, lru_w_igate, lru_b_igate, lru_lambda, lru_w_out, moe_norm, moe_w_group, moe_w_expert, moe_w_gate, moe_w_up, moe_w_down, final_norm):
    bsz, seq, d = x.shape
    x2 = x.reshape(bsz * seq, d)

    x2 = _gla_mixer(x2, gla_norm[0], gla_w_in[0], gla_w_alpha[0], gla_b_alpha[0], gla_head_norm[0],
                    gla_w_out[0], bsz, seq)
    x2 = _hier_moe(x2, moe_norm[0], moe_w_group[0], moe_w_expert[0],
                   moe_w_gate, moe_w_up, moe_w_down, 0, final_norm, False)

    x2 = _lru_mixer(x2, lru_norm[0], lru_w_in[0], lru_conv_w[0], lru_conv_b[0], lru_w_rgate[0], lru_b_rgate[0],
                    lru_w_igate[0], lru_b_igate[0], lru_lambda[0], lru_w_out[0], bsz, seq)
    x2 = _hier_moe(x2, moe_norm[1], moe_w_group[1], moe_w_expert[1],
                   moe_w_gate, moe_w_up, moe_w_down, 1, final_norm, True)
    return x2.reshape(bsz, seq, d)
```
